```python
import math
import jax, jax.numpy as jnp
from jax import lax
import numpy as np

D_MODEL = 2048
BATCH = 16
SEQ = 2048
DEPTH = 1
DEC_BATCH = 1
DEC_SEQ = 8192
PAST_LEN = 128

N_META = 16
GRID_W = 64
H_Q = 16
H_KV = 4
HEAD_DIM = 128
GQA_G = H_Q // H_KV
ATTN_W = H_Q * HEAD_DIM
KV_W = H_KV * HEAD_DIM
ROPE_PAIRS = HEAD_DIM // 4
ROPE_THETA = 10000.0
Q_BLOCK = 128
POOL_W = D_MODEL // 2
N_POOL_GROUPS = 4
POOL_GROUP_W = POOL_W // N_POOL_GROUPS
POOL_WINDOWS = (2, 4, 8, 16)
IN_W = 2 * POOL_W + ATTN_W + 2 * KV_W + ATTN_W + 2 * D_MODEL
EPS = 1e-6

kernel_name = "hybrid_pool_gqa_axialrope_meta_encoder"


def rmsnorm(x, g):
    xf = x.astype(jnp.float32)
    y = xf * lax.rsqrt(jnp.mean(xf * xf, axis=-1, keepdims=True) + EPS)
    return (y * g.astype(jnp.float32)).astype(x.dtype)


def axial_angles(n_tokens):
    rows = n_tokens // GRID_W
    r = jnp.repeat(jnp.arange(rows, dtype=jnp.float32), GRID_W)
    c = jnp.tile(jnp.arange(GRID_W, dtype=jnp.float32), rows)
    pos_r = jnp.concatenate([-jnp.ones((N_META,), jnp.float32), r])
    pos_c = jnp.concatenate([jnp.arange(N_META, dtype=jnp.float32), c])
    inv = ROPE_THETA ** (-jnp.arange(ROPE_PAIRS, dtype=jnp.float32) / ROPE_PAIRS)
    return pos_r[:, None] * inv[None, :], pos_c[:, None] * inv[None, :]


def rope_half(x, ang):
    c = jnp.cos(ang)[None, :, None, :]
    s = jnp.sin(ang)[None, :, None, :]
    x1, x2 = x[..., :ROPE_PAIRS], x[..., ROPE_PAIRS:]
    return jnp.concatenate([x1 * c - x2 * s, x1 * s + x2 * c], axis=-1)


def norm_and_rope(x, g, ang_r, ang_c):
    xf = x.astype(jnp.float32)
    xf = xf * lax.rsqrt(jnp.mean(xf * xf, axis=-1, keepdims=True) + EPS) * g.astype(jnp.float32)
    half = HEAD_DIM // 2
    out = jnp.concatenate([rope_half(xf[..., :half], ang_r), rope_half(xf[..., half:], ang_c)], axis=-1)
    return out.astype(x.dtype)


def multiscale_pool(u):
    L = u.shape[1]
    uf = u.astype(jnp.float32)
    cs = jnp.concatenate([jnp.zeros_like(uf[:, :1]), jnp.cumsum(uf, axis=1)], axis=1)
    t = jnp.arange(L)
    outs = []
    for gi, w in enumerate(POOL_WINDOWS):
        sl = slice(gi * POOL_GROUP_W, (gi + 1) * POOL_GROUP_W)
        lo = jnp.clip(t - w // 2, 0, L)
        hi = jnp.clip(t + w // 2, 0, L)
        seg = cs[:, hi, sl] - cs[:, lo, sl]
        cnt = (hi - lo).astype(jnp.float32)[None, :, None]
        outs.append(seg / cnt - uf[:, :, sl])
    return jnp.concatenate(outs, axis=-1).astype(u.dtype)


def bidir_gqa(q, k, v):
    B, L = q.shape[0], q.shape[1]
    scale = 1.0 / math.sqrt(HEAD_DIM)

    def attend(qb):
        s = jnp.einsum('bqhgd,bkhd->bhgqk', qb, k).astype(jnp.float32) * scale
        p = jax.nn.softmax(s, axis=-1).astype(v.dtype)
        return jnp.einsum('bhgqk,bkhd->bqhgd', p, v)

    qg = q.reshape(B, L, H_KV, GQA_G, HEAD_DIM)
    o_meta = attend(qg[:, :N_META])
    n_real = L - N_META
    nb = n_real // Q_BLOCK
    qr = qg[:, N_META:].reshape(B, nb, Q_BLOCK, H_KV, GQA_G, HEAD_DIM).transpose(1, 0, 2, 3, 4, 5)
    o_real = lax.map(attend, qr)
    o_real = o_real.transpose(1, 0, 2, 3, 4, 5).reshape(B, n_real, H_KV, GQA_G, HEAD_DIM)
    o = jnp.concatenate([o_meta, o_real], axis=1)
    return o.reshape(B, L, ATTN_W)


def encoder(x, meta_tokens, g_norm, w_in, q_norm_g, k_norm_g, pool_mix, pool_scale,
            w_pool_up, w_attn_up, w_out, g_final):
    B, S, _ = x.shape
    L = S + N_META
    h = jnp.concatenate([jnp.broadcast_to(meta_tokens.astype(x.dtype)[None], (B, N_META, D_MODEL)), x], axis=1)
    ang_r, ang_c = axial_angles(S)
    sizes = [POOL_W, POOL_W, ATTN_W, KV_W, KV_W, ATTN_W, D_MODEL]
    idx = [int(v) for v in np.cumsum(sizes)]
    for l in range(DEPTH):
        hn = rmsnorm(h, g_norm[l])
        proj = hn @ w_in[l]
        p_in, p_gate, q, k, v, a_gate, gate_pool, gate_attn = jnp.split(proj, idx, axis=-1)
        pm = multiscale_pool(p_in).reshape(B, L, N_POOL_GROUPS, POOL_GROUP_W)
        pm = jnp.einsum('blgc,gcd->blgd', pm, pool_mix[l]).reshape(B, L, POOL_W)
        pm = pm * pool_scale[l] * jax.nn.silu(p_gate)
        pool_out = pm @ w_pool_up[l]
        q = norm_and_rope(q.reshape(B, L, H_Q, HEAD_DIM), q_norm_g[l], ang_r, ang_c)
        k = norm_and_rope(k.reshape(B, L, H_KV, HEAD_DIM), k_norm_g[l], ang_r, ang_c)
        v = v.reshape(B, L, H_KV, HEAD_DIM)
        ao = bidir_gqa(q, k, v) * jax.nn.silu(a_gate)
        attn_out = ao @ w_attn_up[l]
        merged = jax.nn.sigmoid(gate_pool) * pool_out + jax.nn.sigmoid(gate_attn) * attn_out
        h = h + merged @ w_out[l]
    h = rmsnorm(h, g_final)
    return h[:, N_META:]


def setup_inputs(seed: int = 0) -> dict:
    key = jax.random.key(seed)
    ks = jax.random.split(key, 14)
    f32 = jnp.float32
    nrm = lambda k, shape, s: jax.random.normal(k, shape, f32) * s
    return {
        "x_prompt": nrm(ks[0], (BATCH, SEQ, D_MODEL), 1.0),
        "x_sample": nrm(ks[1], (DEC_BATCH, DEC_SEQ, D_MODEL), 1.0),
        "meta_tokens": nrm(ks[2], (N_META, D_MODEL), 1.0),
        "g_norm": 1.0 + nrm(ks[3], (DEPTH, D_MODEL), 0.05),
        "w_in": nrm(ks[4], (DEPTH, D_MODEL, IN_W), D_MODEL ** -0.5),
        "q_norm_g": 1.0 + nrm(ks[5], (DEPTH, HEAD_DIM), 0.05),
        "k_norm_g": 1.0 + nrm(ks[6], (DEPTH, HEAD_DIM), 0.05),
        "pool_mix": nrm(ks[7], (DEPTH, N_POOL_GROUPS, POOL_GROUP_W, POOL_GROUP_W), POOL_GROUP_W ** -0.5),
        "pool_scale": 1.0 + nrm(ks[8], (DEPTH, POOL_W), 0.05),
        "w_pool_up": nrm(ks[9], (DEPTH, POOL_W, D_MODEL), POOL_W ** -0.5),
        "w_attn_up": nrm(ks[10], (DEPTH, ATTN_W, D_MODEL), ATTN_W ** -0.5),
        "w_out": nrm(ks[11], (DEPTH, D_MODEL, D_MODEL), D_MODEL ** -0.5),
        "g_final": 1.0 + nrm(ks[12], (D_MODEL,), 0.05),
    }


def reference(x_prompt, x_sample, meta_tokens, g_norm, w_in, q_norm_g, k_norm_g, pool_mix,
              pool_scale, w_pool_up, w_attn_up, w_out, g_final):
    y_prompt = encoder(x_prompt, meta_tokens, g_norm, w_in, q_norm_g, k_norm_g, pool_mix,
                       pool_scale, w_pool_up, w_attn_up, w_out, g_final)
    y_sample = encoder(x_sample, meta_tokens, g_norm, w_in, q_norm_g, k_norm_g, pool_mix,
                       pool_scale, w_pool_up, w_attn_up, w_out, g_final)
    return (y_prompt, y_sample)
```

```python
import functools
import math

import jax
import jax.numpy as jnp
from jax import lax
from jax.experimental import pallas as pl
from jax.experimental.pallas import tpu as pltpu

D_MODEL = 2048
N_META = 16
GRID_W = 64
H_Q = 16
H_KV = 4
HEAD_DIM = 128
GQA_G = H_Q // H_KV
ATTN_W = H_Q * HEAD_DIM
KV_W = H_KV * HEAD_DIM
ROPE_PAIRS = HEAD_DIM // 4
ROPE_THETA = 10000.0
POOL_W = D_MODEL // 2
N_POOL_GROUPS = 4
POOL_GROUP_W = POOL_W // N_POOL_GROUPS
POOL_WINDOWS = (2, 4, 8, 16)
POOL_HALO = max(POOL_WINDOWS) // 2
EPS = 1e-6
SCALE = 1.0 / math.sqrt(HEAD_DIM)

C_PIN = 0
C_PGATE = C_PIN + POOL_W
C_Q = C_PGATE + POOL_W
C_K = C_Q + ATTN_W
C_V = C_K + KV_W
C_AGATE = C_V + KV_W
C_GPOOL = C_AGATE + ATTN_W
C_GATTN = C_GPOOL + D_MODEL
IN_W = C_GATTN + D_MODEL
WA_W = C_AGATE
WB_W = IN_W - C_AGATE

VMEM_LIMIT_BYTES = 56 * 1024 * 1024
F32 = jnp.float32
BF16 = jnp.bfloat16


def _const_spec(shape):
    nd = len(shape)
    return pl.BlockSpec(shape, lambda *_: (0,) * nd, pipeline_mode=pl.Buffered(1))


def _dot(a, b):
    return jnp.dot(a, b, preferred_element_type=F32)


def _rmsnorm_rows(x, g):
    ms = jnp.mean(x * x, axis=-1, keepdims=True)
    return x * lax.rsqrt(ms + EPS) * g


def _norm_rope(xh, g, cos_t, sin_lo, sin_hi):
    y = _rmsnorm_rows(xh, g)
    return (y * cos_t + pltpu.roll(y, ROPE_PAIRS, 1) * sin_lo
            + pltpu.roll(y, HEAD_DIM - ROPE_PAIRS, 1) * sin_hi)


def _proj_a_kernel(x_ref, g_ref, w_ref, cos_ref, slo_ref, shi_ref, qg_ref, kg_ref,
                   pin_ref, pgate_ref, q_ref, k_ref, v_ref, hn_ref):
    hn_ref[...] = _rmsnorm_rows(x_ref[...], g_ref[...]).astype(BF16)
    pin_ref[...] = _dot(hn_ref[...], w_ref[:, C_PIN:C_PIN + POOL_W])
    pgate_ref[...] = _dot(hn_ref[...], w_ref[:, C_PGATE:C_PGATE + POOL_W])
    cos_t, sin_lo, sin_hi = cos_ref[...], slo_ref[...], shi_ref[...]
    chunk = GQA_G * HEAD_DIM
    for c in range(ATTN_W // chunk):
        acc = _dot(hn_ref[...], w_ref[:, C_Q + c * chunk:C_Q + (c + 1) * chunk])
        for hh in range(GQA_G):
            lo = hh * HEAD_DIM
            q_ref[:, c * chunk + lo:c * chunk + lo + HEAD_DIM] = _norm_rope(
                acc[:, lo:lo + HEAD_DIM], qg_ref[...], cos_t, sin_lo, sin_hi).astype(BF16)
    acc = _dot(hn_ref[...], w_ref[:, C_K:C_K + KV_W])
    for hh in range(H_KV):
        lo = hh * HEAD_DIM
        k_ref[:, lo:lo + HEAD_DIM] = _norm_rope(
            acc[:, lo:lo + HEAD_DIM], kg_ref[...], cos_t, sin_lo, sin_hi).astype(BF16)
    v_ref[...] = _dot(hn_ref[...], w_ref[:, C_V:C_V + KV_W]).astype(BF16)


def _proj_a(x2d, g_norm, wa, tables, qg, kg, tm, n_seq_tiles):
    t = x2d.shape[0]
    row = lambda w: pl.BlockSpec((tm, w), lambda i: (i, 0))
    tab = pl.BlockSpec((tm, HEAD_DIM), lambda i: (i % n_seq_tiles, 0))
    return pl.pallas_call(
        _proj_a_kernel,
        grid=(t // tm,),
        in_specs=[row(D_MODEL), _const_spec((1, D_MODEL)), _const_spec((D_MODEL, WA_W)),
                  tab, tab, tab, _const_spec((1, HEAD_DIM)), _const_spec((1, HEAD_DIM))],
        out_specs=[row(POOL_W), row(POOL_W), row(ATTN_W), row(KV_W), row(KV_W)],
        out_shape=[jax.ShapeDtypeStruct((t, POOL_W), F32),
                   jax.ShapeDtypeStruct((t, POOL_W), F32),
                   jax.ShapeDtypeStruct((t, ATTN_W), BF16),
                   jax.ShapeDtypeStruct((t, KV_W), BF16),
                   jax.ShapeDtypeStruct((t, KV_W), BF16)],
        scratch_shapes=[pltpu.VMEM((tm, D_MODEL), BF16)],
        compiler_params=pltpu.CompilerParams(
            dimension_semantics=("parallel",), vmem_limit_bytes=VMEM_LIMIT_BYTES),
        name="proj_a",
    )(x2d, g_norm, wa, *tables, qg, kg)


def _proj_b_kernel(x_ref, g_ref, w_ref, ag_ref, gp_ref, ga_ref, hn_ref):
    hn_ref[...] = _rmsnorm_rows(x_ref[...], g_ref[...]).astype(BF16)
    chunk = 512
    for out_ref, base in ((ag_ref, 0), (gp_ref, ATTN_W), (ga_ref, ATTN_W + D_MODEL)):
        for c in range(D_MODEL // chunk):
            out_ref[:, c * chunk:(c + 1) * chunk] = _dot(
                hn_ref[...], w_ref[:, base + c * chunk:base + (c + 1) * chunk])


def _proj_b(x2d, g_norm, wb, tm):
    t = x2d.shape[0]
    row = lambda w: pl.BlockSpec((tm, w), lambda i: (i, 0))
    return pl.pallas_call(
        _proj_b_kernel,
        grid=(t // tm,),
        in_specs=[row(D_MODEL), _const_spec((1, D_MODEL)), _const_spec((D_MODEL, WB_W))],
        out_specs=[row(ATTN_W), row(D_MODEL), row(D_MODEL)],
        out_shape=[jax.ShapeDtypeStruct((t, ATTN_W), F32),
                   jax.ShapeDtypeStruct((t, D_MODEL), F32),
                   jax.ShapeDtypeStruct((t, D_MODEL), F32)],
        scratch_shapes=[pltpu.VMEM((tm, D_MODEL), BF16)],
        compiler_params=pltpu.CompilerParams(
            dimension_semantics=("parallel",), vmem_limit_bytes=VMEM_LIMIT_BYTES),
        name="proj_b",
    )(x2d, g_norm, wb)


_NT = (((1,), (1,)), ((), ()))


def _attn_kernel(q_ref, k_ref, v_ref, km_ref, vm_ref, ag_ref, o_ref,
                 q2_ref, m_ref, l_ref, acc_ref, *, tq, tk, n_chunks):
    for g in range(GQA_G):
        q2_ref[g * tq:(g + 1) * tq, :] = q_ref[:, g * HEAD_DIM:(g + 1) * HEAD_DIM]

    s = lax.dot_general(q2_ref[...], km_ref[...], _NT, preferred_element_type=F32) * SCALE
    m0 = jnp.max(s, axis=-1, keepdims=True)
    p = jnp.exp(s - m0)
    m_ref[...] = m0
    l_ref[...] = jnp.sum(p, axis=-1, keepdims=True)
    acc_ref[...] = _dot(p.astype(BF16), vm_ref[...])

    def body(c, carry):
        off = pl.multiple_of(c * tk, tk)
        s = lax.dot_general(q2_ref[...], k_ref[pl.ds(off, tk), :], _NT,
                            preferred_element_type=F32) * SCALE
        m_prev = m_ref[...]
        m_new = jnp.maximum(m_prev, jnp.max(s, axis=-1, keepdims=True))
        alpha = jnp.exp(m_prev - m_new)
        p = jnp.exp(s - m_new)
        l_ref[...] = alpha * l_ref[...] + jnp.sum(p, axis=-1, keepdims=True)
        acc_ref[...] = alpha * acc_ref[...] + _dot(p.astype(BF16), v_ref[pl.ds(off, tk), :])
        m_ref[...] = m_new
        return carry

    lax.fori_loop(0, n_chunks, body, 0)

    o = acc_ref[...] / l_ref[...]
    for g in range(GQA_G):
        cols = slice(g * HEAD_DIM, (g + 1) * HEAD_DIM)
        o_ref[:, cols] = (o[g * tq:(g + 1) * tq] * jax.nn.silu(ag_ref[:, cols])).astype(BF16)


def _attention(q, k, v, k_meta, v_meta, a_gate, n_batch, seq, tq, tk):
    t = q.shape[0]
    nqt = seq // tq
    gw = GQA_G * HEAD_DIM
    qspec = pl.BlockSpec((tq, gw), lambda b, h, i: (b * nqt + i, h))
    kvspec = pl.BlockSpec((seq, HEAD_DIM), lambda b, h, i: (b, h))
    mspec = pl.BlockSpec((N_META, HEAD_DIM), lambda b, h, i: (0, h))
    kern = functools.partial(_attn_kernel, tq=tq, tk=tk, n_chunks=seq // tk)
    return pl.pallas_call(
        kern,
        grid=(n_batch, H_KV, nqt),
        in_specs=[qspec, kvspec, kvspec, mspec, mspec, qspec],
        out_specs=qspec,
        out_shape=jax.ShapeDtypeStruct((t, ATTN_W), BF16),
        scratch_shapes=[pltpu.VMEM((GQA_G * tq, HEAD_DIM), BF16),
                        pltpu.VMEM((GQA_G * tq, 1), F32),
                        pltpu.VMEM((GQA_G * tq, 1), F32),
                        pltpu.VMEM((GQA_G * tq, HEAD_DIM), F32)],
        compiler_params=pltpu.CompilerParams(
            dimension_semantics=("parallel", "parallel", "parallel"),
            vmem_limit_bytes=VMEM_LIMIT_BYTES),
        name="attn",
    )(q, k, v, k_meta, v_meta, a_gate)


def _merge_kernel(pin_ref, prev_ref, next_ref, pmeta_ref, pgate_ref, ao_ref, gp_ref, ga_ref,
                  x_ref, mix_ref, pscale_ref, wpu_ref, wau_ref, wo_ref, gfin_ref,
                  y_ref, ext_ref, pm_ref, *, tm, n_seq_tiles, seq):
    seq_tile = pl.program_id(0) % n_seq_tiles
    ext_ref[0:POOL_HALO, :] = jnp.where(
        seq_tile == 0, pmeta_ref[N_META - POOL_HALO:N_META, :], prev_ref[...])
    ext_ref[POOL_HALO:POOL_HALO + tm, :] = pin_ref[...]
    ext_ref[POOL_HALO + tm:2 * POOL_HALO + tm, :] = jnp.where(
        seq_tile == n_seq_tiles - 1, 0.0, next_ref[...])

    n = seq_tile * tm + lax.broadcasted_iota(jnp.int32, (tm, 1), 0)
    for gi, w in enumerate(POOL_WINDOWS):
        half = w // 2
        cols = slice(gi * POOL_GROUP_W, (gi + 1) * POOL_GROUP_W)
        seg = ext_ref[POOL_HALO - half:POOL_HALO - half + tm, cols]
        for d in range(-half + 1, half):
            seg = seg + ext_ref[POOL_HALO + d:POOL_HALO + d + tm, cols]
        cnt = jnp.minimum(w, seq - n + half).astype(F32)
        pooled = seg / cnt - pin_ref[:, cols]
        pm_ref[:, cols] = _dot(pooled.astype(BF16), mix_ref[gi])

    pm = pm_ref[...] * pscale_ref[...] * jax.nn.silu(pgate_ref[...])
    pool_out = _dot(pm.astype(BF16), wpu_ref[...])
    attn_out = _dot(ao_ref[...], wau_ref[...])
    merged = jax.nn.sigmoid(gp_ref[...]) * pool_out + jax.nn.sigmoid(ga_ref[...]) * attn_out
    h = x_ref[...] + _dot(merged.astype(BF16), wo_ref[...])
    y_ref[...] = _rmsnorm_rows(h, gfin_ref[...])


def _merge(p_in, p_in_meta, p_gate, ao, g_pool, g_attn, x2d, mix, pscale, wpu, wau, wo, gfin,
           seq, tm):
    t = x2d.shape[0]
    n_seq_tiles = seq // tm
    hb = tm // POOL_HALO
    n_hb = t // POOL_HALO
    row = lambda w: pl.BlockSpec((tm, w), lambda i: (i, 0))
    prev = pl.BlockSpec((POOL_HALO, POOL_W), lambda i: (jnp.maximum(i * hb - 1, 0), 0))
    nxt = pl.BlockSpec((POOL_HALO, POOL_W), lambda i: (jnp.minimum((i + 1) * hb, n_hb - 1), 0))
    kern = functools.partial(_merge_kernel, tm=tm, n_seq_tiles=n_seq_tiles, seq=seq)
    return pl.pallas_call(
        kern,
        grid=(t // tm,),
        in_specs=[row(POOL_W), prev, nxt, _const_spec((N_META, POOL_W)), row(POOL_W),
                  row(ATTN_W), row(D_MODEL), row(D_MODEL), row(D_MODEL),
                  _const_spec((N_POOL_GROUPS, POOL_GROUP_W, POOL_GROUP_W)),
                  _const_spec((1, POOL_W)), _const_spec((POOL_W, D_MODEL)),
                  _const_spec((ATTN_W, D_MODEL)), _const_spec((D_MODEL, D_MODEL)),
                  _const_spec((1, D_MODEL))],
        out_specs=row(D_MODEL),
        out_shape=jax.ShapeDtypeStruct((t, D_MODEL), F32),
        scratch_shapes=[pltpu.VMEM((tm + 2 * POOL_HALO, POOL_W), F32),
                        pltpu.VMEM((tm, POOL_W), F32)],
        compiler_params=pltpu.CompilerParams(
            dimension_semantics=("parallel",), vmem_limit_bytes=VMEM_LIMIT_BYTES),
        name="merge",
    )(p_in, p_in, p_in, p_in_meta, p_gate, ao, g_pool, g_attn, x2d, mix, pscale, wpu, wau, wo,
      gfin)


def _rope_tables(pos_r, pos_c):
    inv = ROPE_THETA ** (-jnp.arange(ROPE_PAIRS, dtype=F32) / ROPE_PAIRS)
    ang_r = pos_r[:, None] * inv[None, :]
    ang_c = pos_c[:, None] * inv[None, :]
    cr, sr, cc, sc = jnp.cos(ang_r), jnp.sin(ang_r), jnp.cos(ang_c), jnp.sin(ang_c)
    z = jnp.zeros_like(sr)
    return (jnp.concatenate([cr, cr, cc, cc], axis=-1),
            jnp.concatenate([z, sr, z, sc], axis=-1),
            jnp.concatenate([-sr, z, -sc, z], axis=-1))


def _seq_tables(seq):
    n = jnp.arange(seq)
    return _rope_tables((n // GRID_W).astype(F32), (n % GRID_W).astype(F32))


def _meta_tables():
    return _rope_tables(-jnp.ones((N_META,), F32), jnp.arange(N_META, dtype=F32))


def _encoder(x, meta_parts, weights, tiles):
    g_norm, wa, wb, qg, kg, mix, pscale, wpu, wau, wo, gfin = weights
    p_in_meta, k_meta, v_meta = meta_parts
    n_batch, seq, _ = x.shape
    x2d = x.reshape(n_batch * seq, D_MODEL)
    tm_a, tm_b, tq, tk, tm_m = tiles
    p_in, p_gate, q, k, v = _proj_a(x2d, g_norm, wa, _seq_tables(seq), qg, kg, tm_a, seq // tm_a)
    a_gate, g_pool, g_attn = _proj_b(x2d, g_norm, wb, tm_b)
    ao = _attention(q, k, v, k_meta, v_meta, a_gate, n_batch, seq, tq, tk)
    y = _merge(p_in, p_in_meta, p_gate, ao, g_pool, g_attn, x2d, mix, pscale, wpu, wau, wo, gfin,
               seq, tm_m)
    return y.reshape(n_batch, seq, D_MODEL)


def kernel(x_prompt, x_sample, meta_tokens, g_norm, w_in, q_norm_g, k_norm_g, pool_mix,
           pool_scale, w_pool_up, w_attn_up, w_out, g_final):
    w_in_b = w_in[0].astype(BF16)
    weights = (g_norm[0][None, :], w_in_b[:, :WA_W], w_in_b[:, WA_W:],
               q_norm_g[0][None, :], k_norm_g[0][None, :], pool_mix[0].astype(BF16),
               pool_scale[0][None, :], w_pool_up[0].astype(BF16), w_attn_up[0].astype(BF16),
               w_out[0].astype(BF16), g_final[None, :])
    g_n, wa, _, qg, kg = weights[0], weights[1], weights[2], weights[3], weights[4]
    p_in_meta, _, _, k_meta, v_meta = _proj_a(
        meta_tokens, g_n, wa, _meta_tables(), qg, kg, N_META, 1)
    meta_parts = (p_in_meta, k_meta, v_meta)
    tiles = (512, 256, 256, 512, 256)
    y_prompt = _encoder(x_prompt, meta_parts, weights, tiles)
    y_sample = _encoder(x_sample, meta_parts, weights, tiles)
    return (y_prompt, y_sample)
```
